```python
import math
import jax, jax.numpy as jnp
from jax import lax
import numpy as np

D_MODEL = 1024
BATCH = 4
SEQ = 8192
DEPTH = 1

PLE_DIM = 256
CONV_WIDTH = 4
RG_WIDTH = D_MODEL // 2
RG_BLOCKS = 8
RG_BLOCK_DIM = RG_WIDTH // RG_BLOCKS
RG_C = 8.0
GDN_HEADS = 4
GDN_DK = 128
GDN_DV = 128
GDN_QK_WIDTH = GDN_HEADS * GDN_DK
GDN_V_WIDTH = GDN_HEADS * GDN_DV
GDN_CHUNK = 64
MIX_WIDTH = RG_WIDTH + GDN_V_WIDTH
IN_COLS = 2 * RG_WIDTH + 2 * GDN_QK_WIDTH + 2 * GDN_V_WIDTH + 2 * GDN_HEADS
D_FF = -(-8 * D_MODEL // (3 * 256)) * 256
EPS = 1e-6

kernel_name = "hybrid_rglru_gdeltanet_block"


def rmsnorm(x, w):
    xf = x.astype(jnp.float32)
    var = jnp.mean(xf * xf, axis=-1, keepdims=True)
    return (xf * lax.rsqrt(var + EPS) * w.astype(jnp.float32)).astype(x.dtype)


def causal_dwconv(u, w):
    k_taps = w.shape[0]
    s = u.shape[1]
    up = jnp.pad(u, ((0, 0), (k_taps - 1, 0), (0, 0)))
    out = up[:, 0:s] * w[0]
    for j in range(1, k_taps):
        out = out + up[:, j:j + s] * w[j]
    return out


def l2norm(t):
    return t * lax.rsqrt(jnp.sum(t * t, axis=-1, keepdims=True) + EPS)


def rglru_group(xa, ga, conv_w, conv_b, wx, bx, wa, ba, lam):
    b, s, _ = xa.shape
    u = causal_dwconv(xa.astype(jnp.float32), conv_w.astype(jnp.float32)) + conv_b.astype(jnp.float32)
    uh = u.reshape(b, s, RG_BLOCKS, RG_BLOCK_DIM)
    gate_x = jax.nn.sigmoid(jnp.einsum('bshi,hij->bshj', uh, wx.astype(jnp.float32)).reshape(b, s, RG_WIDTH) + bx)
    gate_a = jax.nn.sigmoid(jnp.einsum('bshi,hij->bshj', uh, wa.astype(jnp.float32)).reshape(b, s, RG_WIDTH) + ba)
    log_a = -RG_C * gate_a * jax.nn.softplus(-lam.astype(jnp.float32))
    a = jnp.exp(log_a)
    mult = jnp.sqrt(jnp.maximum(-jnp.expm1(2.0 * log_a), 0.0))
    first = (jnp.arange(s) == 0)[None, :, None]
    mult = jnp.where(first, 1.0, mult)
    bt = u * gate_x * mult

    def combine(l, r):
        return (l[0] * r[0], r[0] * l[1] + r[1])

    _, h = lax.associative_scan(combine, (a, bt), axis=1)
    return h * jax.nn.gelu(ga.astype(jnp.float32))


def gated_delta_rule_chunked(q, k, v, g, beta):
    b, s, h, dk = q.shape
    dv = v.shape[-1]
    n = s // GDN_CHUNK

    def chunks(t):
        t = t.reshape((b, n, GDN_CHUNK, h) + t.shape[3:])
        return jnp.moveaxis(t, 3, 1)

    q = chunks(q * (dk ** -0.5))
    k = chunks(k)
    v = chunks(v)
    beta = chunks(beta)
    gc = jnp.cumsum(chunks(g), axis=-1)
    idx = jnp.arange(GDN_CHUNK)
    causal = idx[:, None] >= idx[None, :]
    strict = idx[:, None] > idx[None, :]
    diff = gc[..., :, None] - gc[..., None, :]
    decay = jnp.where(causal, jnp.exp(jnp.where(causal, diff, 0.0)), 0.0)
    kb = k * beta[..., None]
    kk = jnp.einsum('bhnid,bhnjd->bhnij', kb, k) * decay
    a_mat = jnp.where(strict, kk, 0.0) + jnp.eye(GDN_CHUNK, dtype=kk.dtype)
    u = lax.linalg.triangular_solve(a_mat, v * beta[..., None], left_side=True, lower=True, unit_diagonal=True)
    w = lax.linalg.triangular_solve(a_mat, kb * jnp.exp(gc)[..., None], left_side=True, lower=True, unit_diagonal=True)
    qk = jnp.einsum('bhnid,bhnjd->bhnij', q, k) * decay
    q_dec = q * jnp.exp(gc)[..., None]
    k_dec = k * jnp.exp(gc[..., -1:] - gc)[..., None]
    g_tot = jnp.exp(gc[..., -1])

    def step(state, inp):
        w_n, u_n, qd_n, kd_n, qk_n, gt_n = inp
        v_new = u_n - jnp.einsum('bhcd,bhde->bhce', w_n, state)
        o_n = jnp.einsum('bhcd,bhde->bhce', qd_n, state) + jnp.einsum('bhij,bhje->bhie', qk_n, v_new)
        state = state * gt_n[..., None, None] + jnp.einsum('bhcd,bhce->bhde', kd_n, v_new)
        return state, o_n

    xs = (jnp.moveaxis(w, 2, 0), jnp.moveaxis(u, 2, 0), jnp.moveaxis(q_dec, 2, 0),
          jnp.moveaxis(k_dec, 2, 0), jnp.moveaxis(qk, 2, 0), jnp.moveaxis(g_tot, 2, 0))
    state0 = jnp.zeros((b, h, dk, dv), jnp.float32)
    _, o = lax.scan(step, state0, xs)
    return jnp.transpose(o, (1, 0, 3, 2, 4)).reshape(b, s, h, dv)


def gdn_group(q, k, v, z, b_logit, a_logit, conv_w, a_log, dt_bias, norm_w):
    b, s, _ = q.shape
    qkv = jnp.concatenate([q, k, v], axis=-1).astype(jnp.float32)
    qkv = jax.nn.silu(causal_dwconv(qkv, conv_w.astype(jnp.float32)))
    q, k, v = jnp.split(qkv, [GDN_QK_WIDTH, 2 * GDN_QK_WIDTH], axis=-1)
    q = l2norm(q.reshape(b, s, GDN_HEADS, GDN_DK))
    k = l2norm(k.reshape(b, s, GDN_HEADS, GDN_DK))
    v = v.reshape(b, s, GDN_HEADS, GDN_DV)
    beta = jax.nn.sigmoid(b_logit.astype(jnp.float32))
    g = -jnp.exp(a_log.astype(jnp.float32)) * jax.nn.softplus(a_logit.astype(jnp.float32) + dt_bias)
    o = gated_delta_rule_chunked(q, k, v, g, beta)
    var = jnp.mean(o * o, axis=-1, keepdims=True)
    zh = z.astype(jnp.float32).reshape(b, s, GDN_HEADS, GDN_DV)
    o = o * lax.rsqrt(var + EPS) * norm_w.astype(jnp.float32) * jax.nn.silu(zh)
    return o.reshape(b, s, GDN_V_WIDTH)


def setup_inputs(seed: int = 0) -> dict:
    key = jax.random.key(seed)
    ks = jax.random.split(key, 26)
    f32 = jnp.float32
    nrm = lambda k, shape, scale: jax.random.normal(k, shape, f32) * scale
    gain = lambda k, shape: 1.0 + 0.02 * jax.random.normal(k, shape, f32)
    L = DEPTH
    x = jax.random.normal(ks[0], (BATCH, SEQ, D_MODEL), f32)
    p = jax.random.normal(ks[1], (DEPTH, BATCH, SEQ, PLE_DIM), f32)
    a_c = jax.random.uniform(ks[9], (L, RG_WIDTH), f32, 0.9, 0.999)
    sig = a_c ** (1.0 / RG_C)
    rg_lambda = jnp.log(sig) - jnp.log1p(-sig)
    gdn_a_log = jnp.log(jax.random.uniform(ks[11], (L, GDN_HEADS), f32, 1.0, 16.0))
    dt = jnp.exp(jax.random.uniform(ks[12], (L, GDN_HEADS), f32, math.log(1e-3), math.log(1e-1)))
    gdn_dt_bias = dt + jnp.log(-jnp.expm1(-dt))
    return {
        "x": x,
        "p": p,
        "norm_mix_w": gain(ks[2], (L, D_MODEL)),
        "w_in": nrm(ks[3], (L, D_MODEL, IN_COLS), D_MODEL ** -0.5),
        "conv_a_w": nrm(ks[4], (L, CONV_WIDTH, RG_WIDTH), CONV_WIDTH ** -0.5),
        "conv_a_b": nrm(ks[5], (L, RG_WIDTH), 0.01),
        "rg_wx": nrm(ks[6], (L, RG_BLOCKS, RG_BLOCK_DIM, RG_BLOCK_DIM), RG_BLOCK_DIM ** -0.5),
        "rg_bx": nrm(ks[7], (L, RG_WIDTH), 0.01),
        "rg_wa": nrm(ks[8], (L, RG_BLOCKS, RG_BLOCK_DIM, RG_BLOCK_DIM), RG_BLOCK_DIM ** -0.5),
        "rg_ba": nrm(ks[10], (L, RG_WIDTH), 0.01),
        "rg_lambda": rg_lambda,
        "conv_qkv_w": nrm(ks[13], (L, CONV_WIDTH, 2 * GDN_QK_WIDTH + GDN_V_WIDTH), CONV_WIDTH ** -0.5),
        "gdn_a_log": gdn_a_log,
        "gdn_dt_bias": gdn_dt_bias,
        "gdn_norm_w": gain(ks[14], (L, GDN_DV)),
        "w_out": nrm(ks[15], (L, MIX_WIDTH, D_MODEL), MIX_WIDTH ** -0.5),
        "norm_ffn_w": gain(ks[16], (L, D_MODEL)),
        "w_gate": nrm(ks[17], (L, D_MODEL, D_FF), D_MODEL ** -0.5),
        "w_up": nrm(ks[18], (L, D_MODEL, D_FF), D_MODEL ** -0.5),
        "w_down": nrm(ks[19], (L, D_FF, D_MODEL), D_FF ** -0.5),
        "norm_ple_w": gain(ks[20], (L, D_MODEL)),
        "w_ple_gate": nrm(ks[21], (L, D_MODEL, D_MODEL), D_MODEL ** -0.5),
        "b_ple_gate": nrm(ks[22], (L, D_MODEL), 0.01),
        "w_ple_proj": nrm(ks[23], (L, PLE_DIM, D_MODEL), PLE_DIM ** -0.5),
        "norm_final_w": gain(ks[24], (D_MODEL,)),
    }


def reference(x, p, norm_mix_w, w_in, conv_a_w, conv_a_b, rg_wx, rg_bx, rg_wa, rg_ba, rg_lambda,
              conv_qkv_w, gdn_a_log, gdn_dt_bias, gdn_norm_w, w_out, norm_ffn_w, w_gate, w_up, w_down,
              norm_ple_w, w_ple_gate, b_ple_gate, w_ple_proj, norm_final_w):
    splits = [RG_WIDTH, 2 * RG_WIDTH,
              2 * RG_WIDTH + GDN_QK_WIDTH, 2 * RG_WIDTH + 2 * GDN_QK_WIDTH,
              2 * RG_WIDTH + 2 * GDN_QK_WIDTH + GDN_V_WIDTH,
              2 * RG_WIDTH + 2 * GDN_QK_WIDTH + 2 * GDN_V_WIDTH,
              2 * RG_WIDTH + 2 * GDN_QK_WIDTH + 2 * GDN_V_WIDTH + GDN_HEADS]
    for i in range(DEPTH):
        h = rmsnorm(x, norm_mix_w[i])
        proj = jnp.einsum('bsd,dc->bsc', h, w_in[i])
        xa, ga, q, k, v, z, b_logit, a_logit = jnp.split(proj, splits, axis=-1)
        ya = rglru_group(xa, ga, conv_a_w[i], conv_a_b[i], rg_wx[i], rg_bx[i], rg_wa[i], rg_ba[i], rg_lambda[i])
        yb = gdn_group(q, k, v, z, b_logit, a_logit, conv_qkv_w[i], gdn_a_log[i], gdn_dt_bias[i], gdn_norm_w[i])
        y = jnp.concatenate([ya, yb], axis=-1).astype(x.dtype)
        x = x + jnp.einsum('bsm,md->bsd', y, w_out[i])
        h = rmsnorm(x, norm_ffn_w[i])
        ff = jax.nn.silu(jnp.einsum('bsd,df->bsf', h, w_gate[i])) * jnp.einsum('bsd,df->bsf', h, w_up[i])
        x = x + jnp.einsum('bsf,fd->bsd', ff, w_down[i])
        h = rmsnorm(x, norm_ple_w[i])
        gate = jax.nn.sigmoid(jnp.einsum('bsd,de->bse', h, w_ple_gate[i]) + b_ple_gate[i])
        x = x + gate * jnp.einsum('bsp,pd->bsd', p[i], w_ple_proj[i])
    return rmsnorm(x, norm_final_w)
```

```python
import functools

import jax
import jax.numpy as jnp
from jax import lax
from jax.experimental import pallas as pl
from jax.experimental.pallas import tpu as pltpu

F32 = jnp.float32
BF16 = jnp.bfloat16

D_MODEL = 1024
PLE_DIM = 256
CONV_WIDTH = 4
RG_WIDTH = D_MODEL // 2
RG_BLOCKS = 8
RG_C = 8.0
GDN_HEADS = 4
GDN_DK = 128
GDN_DV = 128
GDN_QK_WIDTH = GDN_HEADS * GDN_DK
GDN_V_WIDTH = GDN_HEADS * GDN_DV
GDN_CHUNK = 64
MAIN_COLS = 2 * RG_WIDTH + 2 * GDN_QK_WIDTH + 2 * GDN_V_WIDTH
LANES = 128
SUBLANES = 8
D_FF = -(-8 * D_MODEL // (3 * 256)) * 256
EPS = 1e-6

VMEM_LIMIT = 56 * 1024 * 1024

TM_PROJ = 256
TS_RG = 512
TS_GDN = 256
TM_FFN = 256


def _dot(a, b):
    return jnp.dot(a.astype(BF16), b.astype(BF16), preferred_element_type=F32)


def _split3(x):
    hi = x.astype(BF16)
    r1 = x - hi.astype(F32)
    mid = r1.astype(BF16)
    lo = (r1 - mid.astype(F32)).astype(BF16)
    return hi, mid, lo


def _dot_exact_lhs(mask_bf16, x):
    hi, mid, lo = _split3(x)
    acc = jnp.dot(mask_bf16, lo, preferred_element_type=F32)
    acc = acc + jnp.dot(mask_bf16, mid, preferred_element_type=F32)
    return acc + jnp.dot(mask_bf16, hi, preferred_element_type=F32)


def _rmsnorm(x, w):
    var = jnp.mean(x * x, axis=-1, keepdims=True)
    return x * lax.rsqrt(var + EPS) * w


def _sigmoid(x):
    return 1.0 / (1.0 + jnp.exp(-x))


def _silu(x):
    return x * _sigmoid(x)


def _softplus(x):
    return jnp.maximum(x, 0.0) + jnp.log1p(jnp.exp(-jnp.abs(x)))


def _gelu_tanh(x):
    return 0.5 * x * (1.0 + jnp.tanh(0.7978845608028654 * (x + 0.044715 * (x * x * x))))


def _causal_conv(buf_ref, x, w_ref, ts):
    buf_ref[pl.ds(SUBLANES, ts), :] = x
    out = x * w_ref[CONV_WIDTH - 1:CONV_WIDTH, :]
    for j in range(CONV_WIDTH - 1):
        shift = CONV_WIDTH - 1 - j
        out = out + buf_ref[pl.ds(SUBLANES - shift, ts), :] * w_ref[j:j + 1, :]
    buf_ref[pl.ds(0, SUBLANES), :] = x[ts - SUBLANES:, :]
    return out


def _inproj_kernel(x_ref, nw_ref, wm_ref, ws_ref, main_ref, small_ref):
    h = _rmsnorm(x_ref[...], nw_ref[...]).astype(BF16)
    main_ref[...] = jnp.dot(h, wm_ref[...], preferred_element_type=F32)
    small_ref[...] = jnp.dot(h, ws_ref[...], preferred_element_type=F32)


def _const_spec(shape):
    nd = len(shape)
    return pl.BlockSpec(shape, lambda *_: (0,) * nd, pipeline_mode=pl.Buffered(1))


def _inproj(x2, nw, w_main, w_small):
    t = x2.shape[0]
    return pl.pallas_call(
        _inproj_kernel,
        grid=(t // TM_PROJ,),
        in_specs=[
            pl.BlockSpec((TM_PROJ, D_MODEL), lambda i: (i, 0)),
            _const_spec((1, D_MODEL)),
            _const_spec((D_MODEL, MAIN_COLS)),
            _const_spec((D_MODEL, LANES)),
        ],
        out_specs=[
            pl.BlockSpec((TM_PROJ, MAIN_COLS), lambda i: (i, 0)),
            pl.BlockSpec((TM_PROJ, LANES), lambda i: (i, 0)),
        ],
        out_shape=[
            jax.ShapeDtypeStruct((t, MAIN_COLS), F32),
            jax.ShapeDtypeStruct((t, LANES), F32),
        ],
        compiler_params=pltpu.CompilerParams(
            dimension_semantics=("arbitrary",), vmem_limit_bytes=VMEM_LIMIT),
        name="inproj",
    )(x2, nw, w_main, w_small)


def _rglru_kernel(xa_ref, ga_ref, cw_ref, cb_ref, wx_ref, wa_ref, bx_ref, ba_ref, lam_ref,
                  out_ref, buf_ref, h_ref, a_ref, b_ref, al_ref, bl_ref):
    ts = TS_RG
    nblk = ts // SUBLANES
    s = pl.program_id(1)

    @pl.when(s == 0)
    def _():
        buf_ref[pl.ds(0, SUBLANES), :] = jnp.zeros((SUBLANES, RG_WIDTH), F32)
        h_ref[...] = jnp.zeros((SUBLANES, RG_WIDTH), F32)

    u = _causal_conv(buf_ref, xa_ref[...], cw_ref, ts) + cb_ref[...]
    ub = u.astype(BF16)
    half = RG_WIDTH // 2

    def gate(w_ref, b_ref_):
        lo = jnp.dot(ub[:, :half], w_ref[0], preferred_element_type=F32)
        hi = jnp.dot(ub[:, half:], w_ref[1], preferred_element_type=F32)
        return _sigmoid(jnp.concatenate([lo, hi], axis=1) + b_ref_[...])

    gate_x = gate(wx_ref, bx_ref)
    gate_a = gate(wa_ref, ba_ref)
    log_a = (-RG_C) * gate_a * _softplus(-lam_ref[...])
    a = jnp.exp(log_a)
    mult = jnp.sqrt(jnp.maximum(-jnp.tanh(log_a) * (a * a + 1.0), 0.0))
    row = lax.broadcasted_iota(jnp.int32, (ts, 1), 0)
    mult = jnp.where(jnp.logical_and(row == 0, s == 0), 1.0, mult)
    bt = u * gate_x * mult

    a3 = a.reshape(nblk, SUBLANES, RG_WIDTH)
    b3 = bt.reshape(nblk, SUBLANES, RG_WIDTH)
    ridx = lax.broadcasted_iota(jnp.int32, (nblk, SUBLANES, RG_WIDTH), 1)
    for sh in (1, 2, 4):
        a_s = pltpu.roll(a3, sh, 1)
        b_s = pltpu.roll(b3, sh, 1)
        m = ridx >= sh
        b3 = jnp.where(m, a3 * b_s + b3, b3)
        a3 = jnp.where(m, a3 * a_s, a3)
    a_ref[...] = a3.reshape(ts, RG_WIDTH)
    b_ref[...] = b3.reshape(ts, RG_WIDTH)
    al_ref[...] = jnp.broadcast_to(a3[:, SUBLANES - 1:, :], a3.shape).reshape(ts, RG_WIDTH)
    bl_ref[...] = jnp.broadcast_to(b3[:, SUBLANES - 1:, :], b3.shape).reshape(ts, RG_WIDTH)

    def body(g, h):
        r = pl.ds(pl.multiple_of(g * SUBLANES, SUBLANES), SUBLANES)
        b_ref[r, :] = a_ref[r, :] * h + b_ref[r, :]
        return al_ref[r, :] * h + bl_ref[r, :]

    h_ref[...] = lax.fori_loop(0, nblk, body, h_ref[...], unroll=8)
    out_ref[...] = (b_ref[...] * _gelu_tanh(ga_ref[...])).astype(out_ref.dtype)


def _rglru(main, batch, seq, cw, cb, wx, wa, bx, ba, lam):
    ns = seq // TS_RG
    vec = lambda: _const_spec((1, RG_WIDTH))
    return pl.pallas_call(
        _rglru_kernel,
        grid=(batch, ns),
        in_specs=[
            pl.BlockSpec((TS_RG, RG_WIDTH), lambda b, s: (b * ns + s, 0)),
            pl.BlockSpec((TS_RG, RG_WIDTH), lambda b, s: (b * ns + s, 1)),
            _const_spec((CONV_WIDTH, RG_WIDTH)),
            vec(),
            _const_spec((2, RG_WIDTH // 2, RG_WIDTH // 2)),
            _const_spec((2, RG_WIDTH // 2, RG_WIDTH // 2)),
            vec(), vec(), vec(),
        ],
        out_specs=pl.BlockSpec((TS_RG, RG_WIDTH), lambda b, s: (b * ns + s, 0)),
        out_shape=jax.ShapeDtypeStruct((batch * seq, RG_WIDTH), BF16),
        scratch_shapes=[
            pltpu.VMEM((TS_RG + SUBLANES, RG_WIDTH), F32),
            pltpu.VMEM((SUBLANES, RG_WIDTH), F32),
            pltpu.VMEM((TS_RG, RG_WIDTH), F32),
            pltpu.VMEM((TS_RG, RG_WIDTH), F32),
            pltpu.VMEM((TS_RG, RG_WIDTH), F32),
            pltpu.VMEM((TS_RG, RG_WIDTH), F32),
        ],
        compiler_params=pltpu.CompilerParams(
            dimension_semantics=("arbitrary", "arbitrary"), vmem_limit_bytes=VMEM_LIMIT),
        name="rglru",
    )(main, main, cw, cb, wx, wa, bx, ba, lam)


def _unit_lower_inverse(l_mat, eye):
    p = eye - l_mat
    m = l_mat
    for _ in range(5):
        m = _dot(m, m)
        p = p + _dot(p, m)
    return p


def _gdn_kernel(q_ref, k_ref, v_ref, z_ref, sm_ref, cwq_ref, cwk_ref, cwv_ref, arow_ref, dtrow_ref,
                nw_ref, out_ref, qbuf, kbuf, vbuf, state_ref):
    ts = TS_GDN
    nchunk = ts // GDN_CHUNK
    s = pl.program_id(1)

    @pl.when(s == 0)
    def _():
        zero = jnp.zeros((SUBLANES, GDN_QK_WIDTH), F32)
        qbuf[pl.ds(0, SUBLANES), :] = zero
        kbuf[pl.ds(0, SUBLANES), :] = zero
        vbuf[pl.ds(0, SUBLANES), :] = zero
        state_ref[...] = jnp.zeros(state_ref.shape, F32)

    q_all = _silu(_causal_conv(qbuf, q_ref[...], cwq_ref, ts))
    k_all = _silu(_causal_conv(kbuf, k_ref[...], cwk_ref, ts))
    v_all = _silu(_causal_conv(vbuf, v_ref[...], cwv_ref, ts))

    sm = sm_ref[...]
    lane = lax.broadcasted_iota(jnp.int32, sm.shape, 1)
    bg = jnp.where(lane < GDN_HEADS, _sigmoid(sm),
                   -jnp.exp(arow_ref[...]) * _softplus(sm + dtrow_ref[...]))

    ri = lax.broadcasted_iota(jnp.int32, (ts, ts), 0)
    ci = lax.broadcasted_iota(jnp.int32, (ts, ts), 1)
    same = (ri // GDN_CHUNK) == (ci // GDN_CHUNK)
    causal = jnp.logical_and(same, ri >= ci)
    strict = jnp.logical_and(same, ri > ci)
    eye = (ri == ci).astype(F32)
    gc_all = _dot_exact_lhs(causal.astype(BF16), bg)
    grest_all = _dot_exact_lhs(jnp.logical_and(same, ri < ci).astype(BF16), bg)
    gc_rows = gc_all.T

    for h in range(GDN_HEADS):
        cols = slice(h * GDN_DK, (h + 1) * GDN_DK)
        qh = q_all[:, cols]
        kh = k_all[:, cols]
        vh = v_all[:, cols]
        qh = qh * lax.rsqrt(jnp.sum(qh * qh, axis=-1, keepdims=True) + EPS) * (GDN_DK ** -0.5)
        kh = kh * lax.rsqrt(jnp.sum(kh * kh, axis=-1, keepdims=True) + EPS)
        beta = bg[:, h:h + 1]
        gc = gc_all[:, GDN_HEADS + h:GDN_HEADS + h + 1]
        grest = grest_all[:, GDN_HEADS + h:GDN_HEADS + h + 1]
        gc_row = gc_rows[GDN_HEADS + h:GDN_HEADS + h + 1, :]
        e_gc = jnp.exp(gc)
        decay = jnp.where(causal, jnp.exp(jnp.where(causal, gc - gc_row, 0.0)), 0.0)
        kb = kh * beta
        kh16 = kh.astype(BF16)
        nt = (((1,), (1,)), ((), ()))
        kk = lax.dot_general(kb.astype(BF16), kh16, nt, preferred_element_type=F32) * decay
        qk = lax.dot_general(qh.astype(BF16), kh16, nt, preferred_element_type=F32) * decay
        t_inv = _unit_lower_inverse(jnp.where(strict, kk, 0.0), eye)
        uw = _dot(t_inv, jnp.concatenate([vh * beta, kb * e_gc], axis=1))
        u_all = uw[:, :GDN_DV]
        w_all = uw[:, GDN_DV:]
        q_dec = qh * e_gc
        k_dec = kh * jnp.exp(grest)
        g_tot = jnp.exp(gc + grest)

        state = state_ref[h]
        outs = []
        for c in range(nchunk):
            rows = slice(c * GDN_CHUNK, (c + 1) * GDN_CHUNK)
            lhs = jnp.concatenate([w_all[rows], q_dec[rows]], axis=0)
            ws = _dot(lhs, state)
            v_new = u_all[rows] - ws[:GDN_CHUNK]
            o_c = ws[GDN_CHUNK:] + _dot(qk[rows, rows], v_new)
            tn = (((0,), (0,)), ((), ()))
            upd = lax.dot_general(k_dec[rows].astype(BF16), v_new.astype(BF16), tn,
                                  preferred_element_type=F32)
            state = state * g_tot[c * GDN_CHUNK:c * GDN_CHUNK + 1, :] + upd
            outs.append(o_c)
        state_ref[h] = state
        o = jnp.concatenate(outs, axis=0)
        var = jnp.mean(o * o, axis=-1, keepdims=True)
        o = o * lax.rsqrt(var + EPS) * nw_ref[...] * _silu(z_ref[:, cols])
        out_ref[:, cols] = o.astype(out_ref.dtype)


def _gdn(main, small, batch, seq, cwq, cwk, cwv, arow, dtrow, nw):
    ns = seq // TS_GDN
    col = lambda c: pl.BlockSpec((TS_GDN, GDN_QK_WIDTH), lambda b, s: (b * ns + s, c))
    return pl.pallas_call(
        _gdn_kernel,
        grid=(batch, ns),
        in_specs=[
            col(2), col(3), col(4), col(5),
            pl.BlockSpec((TS_GDN, LANES), lambda b, s: (b * ns + s, 0)),
            _const_spec((CONV_WIDTH, GDN_QK_WIDTH)),
            _const_spec((CONV_WIDTH, GDN_QK_WIDTH)),
            _const_spec((CONV_WIDTH, GDN_V_WIDTH)),
            _const_spec((1, LANES)),
            _const_spec((1, LANES)),
            _const_spec((1, GDN_DV)),
        ],
        out_specs=pl.BlockSpec((TS_GDN, GDN_V_WIDTH), lambda b, s: (b * ns + s, 0)),
        out_shape=jax.ShapeDtypeStruct((batch * seq, GDN_V_WIDTH), BF16),
        scratch_shapes=[
            pltpu.VMEM((TS_GDN + SUBLANES, GDN_QK_WIDTH), F32),
            pltpu.VMEM((TS_GDN + SUBLANES, GDN_QK_WIDTH), F32),
            pltpu.VMEM((TS_GDN + SUBLANES, GDN_V_WIDTH), F32),
            pltpu.VMEM((GDN_HEADS, GDN_DK, GDN_DV), F32),
        ],
        compiler_params=pltpu.CompilerParams(
            dimension_semantics=("arbitrary", "arbitrary"), vmem_limit_bytes=VMEM_LIMIT),
        name="gdn",
    )(main, main, main, main, small, cwq, cwk, cwv, arow, dtrow, nw)


def _ffn_kernel(x_ref, ya_ref, yb_ref, p_ref, woa_ref, wob_ref, nf_ref, wg_ref, wu_ref, wd_ref,
                np_ref, wpg_ref, bpg_ref, wpp_ref, nfin_ref, out_ref, *, final_norm):
    x = x_ref[...]
    x = x + jnp.dot(ya_ref[...], woa_ref[...], preferred_element_type=F32) \
          + jnp.dot(yb_ref[...], wob_ref[...], preferred_element_type=F32)
    h = _rmsnorm(x, nf_ref[...]).astype(BF16)
    g = jnp.dot(h, wg_ref[...], preferred_element_type=F32)
    u = jnp.dot(h, wu_ref[...], preferred_element_type=F32)
    x = x + _dot(_silu(g) * u, wd_ref[...])
    h = _rmsnorm(x, np_ref[...]).astype(BF16)
    gate = _sigmoid(jnp.dot(h, wpg_ref[...], preferred_element_type=F32) + bpg_ref[...])
    x = x + gate * _dot(p_ref[...], wpp_ref[...])
    out_ref[...] = _rmsnorm(x, nfin_ref[...]) if final_norm else x


def _ffn(x2, ya, yb, p2, woa, wob, nf, wg, wu, wd, npw, wpg, bpg, wpp, nfin, final_norm):
    t = x2.shape[0]
    tile = lambda c: pl.BlockSpec((TM_FFN, c), lambda i: (i, 0))
    return pl.pallas_call(
        functools.partial(_ffn_kernel, final_norm=final_norm),
        grid=(t // TM_FFN,),
        in_specs=[
            tile(D_MODEL), tile(RG_WIDTH), tile(GDN_V_WIDTH), tile(PLE_DIM),
            _const_spec((RG_WIDTH, D_MODEL)), _const_spec((GDN_V_WIDTH, D_MODEL)),
            _const_spec((1, D_MODEL)),
            _const_spec((D_MODEL, D_FF)), _const_spec((D_MODEL, D_FF)), _const_spec((D_FF, D_MODEL)),
            _const_spec((1, D_MODEL)),
            _const_spec((D_MODEL, D_MODEL)), _const_spec((1, D_MODEL)),
            _const_spec((PLE_DIM, D_MODEL)),
            _const_spec((1, D_MODEL)),
        ],
        out_specs=tile(D_MODEL),
        out_shape=jax.ShapeDtypeStruct((t, D_MODEL), F32),
        compiler_params=pltpu.CompilerParams(
            dimension_semantics=("arbitrary",), vmem_limit_bytes=VMEM_LIMIT),
        name="ffn",
    )(x2, ya, yb, p2, woa, wob, nf, wg, wu, wd, npw, wpg, bpg, wpp, nfin)


def _block_diag_halves(w):
    nb, d, _ = w.shape
    per = nb // 2
    eye = jnp.eye(per, dtype=w.dtype)
    w4 = w.reshape(2, per, d, d)
    return jnp.einsum("hbij,bc->hbicj", w4, eye).reshape(2, per * d, per * d)


def kernel(x, p, norm_mix_w, w_in, conv_a_w, conv_a_b, rg_wx, rg_bx, rg_wa, rg_ba, rg_lambda,
           conv_qkv_w, gdn_a_log, gdn_dt_bias, gdn_norm_w, w_out, norm_ffn_w, w_gate, w_up, w_down,
           norm_ple_w, w_ple_gate, b_ple_gate, w_ple_proj, norm_final_w):
    batch, seq, _ = x.shape
    depth = w_in.shape[0]
    t = batch * seq
    x2 = x.reshape(t, D_MODEL)
    row = lambda v: v.reshape(1, -1).astype(F32)
    for i in range(depth):
        w_main = w_in[i][:, :MAIN_COLS].astype(BF16)
        w_small = jnp.pad(w_in[i][:, MAIN_COLS:], ((0, 0), (0, LANES - 2 * GDN_HEADS))).astype(BF16)
        main, small = _inproj(x2, row(norm_mix_w[i]), w_main, w_small)

        ya = _rglru(main, batch, seq, conv_a_w[i], row(conv_a_b[i]),
                    _block_diag_halves(rg_wx[i]).astype(BF16), _block_diag_halves(rg_wa[i]).astype(BF16),
                    row(rg_bx[i]), row(rg_ba[i]), row(rg_lambda[i]))

        cw = conv_qkv_w[i]
        arow = jnp.pad(gdn_a_log[i], (GDN_HEADS, LANES - 2 * GDN_HEADS)).reshape(1, LANES)
        dtrow = jnp.pad(gdn_dt_bias[i], (GDN_HEADS, LANES - 2 * GDN_HEADS)).reshape(1, LANES)
        yb = _gdn(main, small, batch, seq,
                  cw[:, :GDN_QK_WIDTH], cw[:, GDN_QK_WIDTH:2 * GDN_QK_WIDTH], cw[:, 2 * GDN_QK_WIDTH:],
                  arow, dtrow, row(gdn_norm_w[i]))

        x2 = _ffn(x2, ya, yb, p[i].reshape(t, PLE_DIM),
                  w_out[i][:RG_WIDTH].astype(BF16), w_out[i][RG_WIDTH:].astype(BF16),
                  row(norm_ffn_w[i]), w_gate[i].astype(BF16), w_up[i].astype(BF16), w_down[i].astype(BF16),
                  row(norm_ple_w[i]), w_ple_gate[i].astype(BF16), row(b_ple_gate[i]),
                  w_ple_proj[i].astype(BF16), row(norm_final_w), i == depth - 1)
    return x2.reshape(batch, seq, D_MODEL)
```

```python
import functools

import jax
import jax.numpy as jnp
from jax import lax
from jax.experimental import pallas as pl
from jax.experimental.pallas import tpu as pltpu

F32 = jnp.float32
BF16 = jnp.bfloat16

D_MODEL = 1024
PLE_DIM = 256
CONV_WIDTH = 4
RG_WIDTH = D_MODEL // 2
RG_BLOCKS = 8
RG_C = 8.0
GDN_HEADS = 4
GDN_DK = 128
GDN_DV = 128
GDN_QK_WIDTH = GDN_HEADS * GDN_DK
GDN_V_WIDTH = GDN_HEADS * GDN_DV
GDN_CHUNK = 64
MAIN_COLS = 2 * RG_WIDTH + 2 * GDN_QK_WIDTH + 2 * GDN_V_WIDTH
LANES = 128
SUBLANES = 8
D_FF = -(-8 * D_MODEL // (3 * 256)) * 256
EPS = 1e-6

VMEM_LIMIT = 56 * 1024 * 1024

TM_PROJ = 256
TS_RG = 512
TS_GDN = 256
TM_FFN = 256


def _dot(a, b):
    return jnp.dot(a.astype(BF16), b.astype(BF16), preferred_element_type=F32)


def _split3(x):
    hi = x.astype(BF16)
    r1 = x - hi.astype(F32)
    mid = r1.astype(BF16)
    lo = (r1 - mid.astype(F32)).astype(BF16)
    return hi, mid, lo


def _dot_exact_lhs(mask_bf16, x):
    hi, mid, lo = _split3(x)
    acc = jnp.dot(mask_bf16, lo, preferred_element_type=F32)
    acc = acc + jnp.dot(mask_bf16, mid, preferred_element_type=F32)
    return acc + jnp.dot(mask_bf16, hi, preferred_element_type=F32)


def _rmsnorm(x, w):
    var = jnp.mean(x * x, axis=-1, keepdims=True)
    return x * lax.rsqrt(var + EPS) * w


def _sigmoid(x):
    return 1.0 / (1.0 + jnp.exp(-x))


def _silu(x):
    return x * _sigmoid(x)


def _softplus(x):
    return jnp.maximum(x, 0.0) + jnp.log1p(jnp.exp(-jnp.abs(x)))


def _gelu_tanh(x):
    return 0.5 * x * (1.0 + jnp.tanh(0.7978845608028654 * (x + 0.044715 * (x * x * x))))


def _causal_conv(buf_ref, x, w_ref, ts):
    buf_ref[pl.ds(SUBLANES, ts), :] = x
    out = x * w_ref[CONV_WIDTH - 1:CONV_WIDTH, :]
    for j in range(CONV_WIDTH - 1):
        shift = CONV_WIDTH - 1 - j
        out = out + buf_ref[pl.ds(SUBLANES - shift, ts), :] * w_ref[j:j + 1, :]
    buf_ref[pl.ds(0, SUBLANES), :] = x[ts - SUBLANES:, :]
    return out


def _inproj_kernel(x_ref, nw_ref, wm_ref, ws_ref, main_ref, small_ref):
    h = _rmsnorm(x_ref[...], nw_ref[...]).astype(BF16)
    main_ref[...] = jnp.dot(h, wm_ref[...], preferred_element_type=F32)
    small_ref[...] = jnp.dot(h, ws_ref[...], preferred_element_type=F32)


def _const_spec(shape):
    nd = len(shape)
    return pl.BlockSpec(shape, lambda *_: (0,) * nd, pipeline_mode=pl.Buffered(1))


def _inproj(x2, nw, w_main, w_small):
    t = x2.shape[0]
    return pl.pallas_call(
        _inproj_kernel,
        grid=(t // TM_PROJ,),
        in_specs=[
            pl.BlockSpec((TM_PROJ, D_MODEL), lambda i: (i, 0)),
            _const_spec((1, D_MODEL)),
            _const_spec((D_MODEL, MAIN_COLS)),
            _const_spec((D_MODEL, LANES)),
        ],
        out_specs=[
            pl.BlockSpec((TM_PROJ, MAIN_COLS), lambda i: (i, 0)),
            pl.BlockSpec((TM_PROJ, LANES), lambda i: (i, 0)),
        ],
        out_shape=[
            jax.ShapeDtypeStruct((t, MAIN_COLS), F32),
            jax.ShapeDtypeStruct((t, LANES), F32),
        ],
        compiler_params=pltpu.CompilerParams(
            dimension_semantics=("arbitrary",), vmem_limit_bytes=VMEM_LIMIT),
        name="inproj",
    )(x2, nw, w_main, w_small)


def _rglru_kernel(xa_ref, ga_ref, cw_ref, cb_ref, wx_ref, wa_ref, bx_ref, ba_ref, lam_ref,
                  out_ref, buf_ref, h_ref, a_ref, b_ref, al_ref, bl_ref):
    ts = TS_RG
    nblk = ts // SUBLANES
    s = pl.program_id(1)

    @pl.when(s == 0)
    def _():
        buf_ref[pl.ds(0, SUBLANES), :] = jnp.zeros((SUBLANES, RG_WIDTH), F32)
        h_ref[...] = jnp.zeros((SUBLANES, RG_WIDTH), F32)

    u = _causal_conv(buf_ref, xa_ref[...], cw_ref, ts) + cb_ref[...]
    ub = u.astype(BF16)
    half = RG_WIDTH // 2

    def gate(w_ref, b_ref_):
        lo = jnp.dot(ub[:, :half], w_ref[0], preferred_element_type=F32)
        hi = jnp.dot(ub[:, half:], w_ref[1], preferred_element_type=F32)
        return _sigmoid(jnp.concatenate([lo, hi], axis=1) + b_ref_[...])

    gate_x = gate(wx_ref, bx_ref)
    gate_a = gate(wa_ref, ba_ref)
    log_a = (-RG_C) * gate_a * _softplus(-lam_ref[...])
    a = jnp.exp(log_a)
    mult = jnp.sqrt(jnp.maximum(-jnp.tanh(log_a) * (a * a + 1.0), 0.0))
    row = lax.broadcasted_iota(jnp.int32, (ts, 1), 0)
    mult = jnp.where(jnp.logical_and(row == 0, s == 0), 1.0, mult)
    bt = u * gate_x * mult

    a3 = a.reshape(nblk, SUBLANES, RG_WIDTH)
    b3 = bt.reshape(nblk, SUBLANES, RG_WIDTH)
    ridx = lax.broadcasted_iota(jnp.int32, (nblk, SUBLANES, RG_WIDTH), 1)
    for sh in (1, 2, 4):
        a_s = pltpu.roll(a3, sh, 1)
        b_s = pltpu.roll(b3, sh, 1)
        m = ridx >= sh
        b3 = jnp.where(m, a3 * b_s + b3, b3)
        a3 = jnp.where(m, a3 * a_s, a3)
    a_ref[...] = a3.reshape(ts, RG_WIDTH)
    b_ref[...] = b3.reshape(ts, RG_WIDTH)
    al_ref[...] = jnp.broadcast_to(a3[:, SUBLANES - 1:, :], a3.shape).reshape(ts, RG_WIDTH)
    bl_ref[...] = jnp.broadcast_to(b3[:, SUBLANES - 1:, :], b3.shape).reshape(ts, RG_WIDTH)

    def body(g, h):
        r = pl.ds(pl.multiple_of(g * SUBLANES, SUBLANES), SUBLANES)
        b_ref[r, :] = a_ref[r, :] * h + b_ref[r, :]
        return al_ref[r, :] * h + bl_ref[r, :]

    h_ref[...] = lax.fori_loop(0, nblk, body, h_ref[...], unroll=8)
    out_ref[...] = (b_ref[...] * _gelu_tanh(ga_ref[...])).astype(out_ref.dtype)


def _rglru(main, batch, seq, cw, cb, wx, wa, bx, ba, lam):
    ns = seq // TS_RG
    vec = lambda: _const_spec((1, RG_WIDTH))
    return pl.pallas_call(
        _rglru_kernel,
        grid=(batch, ns),
        in_specs=[
            pl.BlockSpec((TS_RG, RG_WIDTH), lambda b, s: (b * ns + s, 0)),
            pl.BlockSpec((TS_RG, RG_WIDTH), lambda b, s: (b * ns + s, 1)),
            _const_spec((CONV_WIDTH, RG_WIDTH)),
            vec(),
            _const_spec((2, RG_WIDTH // 2, RG_WIDTH // 2)),
            _const_spec((2, RG_WIDTH // 2, RG_WIDTH // 2)),
            vec(), vec(), vec(),
        ],
        out_specs=pl.BlockSpec((TS_RG, RG_WIDTH), lambda b, s: (b * ns + s, 0)),
        out_shape=jax.ShapeDtypeStruct((batch * seq, RG_WIDTH), BF16),
        scratch_shapes=[
            pltpu.VMEM((TS_RG + SUBLANES, RG_WIDTH), F32),
            pltpu.VMEM((SUBLANES, RG_WIDTH), F32),
            pltpu.VMEM((TS_RG, RG_WIDTH), F32),
            pltpu.VMEM((TS_RG, RG_WIDTH), F32),
            pltpu.VMEM((TS_RG, RG_WIDTH), F32),
            pltpu.VMEM((TS_RG, RG_WIDTH), F32),
        ],
        compiler_params=pltpu.CompilerParams(
            dimension_semantics=("arbitrary", "arbitrary"), vmem_limit_bytes=VMEM_LIMIT),
        name="rglru",
    )(main, main, cw, cb, wx, wa, bx, ba, lam)


def _unit_lower_inverse(l_mat, eye):
    p = eye - l_mat
    m = l_mat
    for _ in range(5):
        m = _dot(m, m)
        p = p + _dot(p, m)
    return p


def _gdn_kernel(q_ref, k_ref, v_ref, z_ref, sm_ref, cwq_ref, cwk_ref, cwv_ref, arow_ref, dtrow_ref,
                nw_ref, out_ref, qbuf, kbuf, vbuf, state_ref):
    ts = TS_GDN
    nchunk = ts // GDN_CHUNK
    s = pl.program_id(1)

    @pl.when(s == 0)
    def _():
        zero = jnp.zeros((SUBLANES, GDN_QK_WIDTH), F32)
        qbuf[pl.ds(0, SUBLANES), :] = zero
        kbuf[pl.ds(0, SUBLANES), :] = zero
        vbuf[pl.ds(0, SUBLANES), :] = zero
        state_ref[...] = jnp.zeros(state_ref.shape, F32)

    q_all = _silu(_causal_conv(qbuf, q_ref[...], cwq_ref, ts))
    k_all = _silu(_causal_conv(kbuf, k_ref[...], cwk_ref, ts))
    v_all = _silu(_causal_conv(vbuf, v_ref[...], cwv_ref, ts))

    sm = sm_ref[...]
    lane = lax.broadcasted_iota(jnp.int32, sm.shape, 1)
    bg = jnp.where(lane < GDN_HEADS, _sigmoid(sm),
                   -jnp.exp(arow_ref[...]) * _softplus(sm + dtrow_ref[...]))

    ri = lax.broadcasted_iota(jnp.int32, (ts, ts), 0)
    ci = lax.broadcasted_iota(jnp.int32, (ts, ts), 1)
    same = (ri // GDN_CHUNK) == (ci // GDN_CHUNK)
    causal = jnp.logical_and(same, ri >= ci)
    strict = jnp.logical_and(same, ri > ci)
    eye = (ri == ci).astype(F32)
    gc_all = _dot_exact_lhs(causal.astype(BF16), bg)
    grest_all = _dot_exact_lhs(jnp.logical_and(same, ri < ci).astype(BF16), bg)
    gc_rows = gc_all.T

    heads = range(GDN_HEADS)
    cols = [slice(h * GDN_DK, (h + 1) * GDN_DK) for h in heads]
    nt = (((1,), (1,)), ((), ()))
    tn = (((0,), (0,)), ((), ()))

    q, k, kb, kk, qk, rhs, e_gc, grest = [], [], [], [], [], [], [], []
    for h in heads:
        qh = q_all[:, cols[h]]
        kh = k_all[:, cols[h]]
        qh = qh * lax.rsqrt(jnp.sum(qh * qh, axis=-1, keepdims=True) + EPS) * (GDN_DK ** -0.5)
        kh = kh * lax.rsqrt(jnp.sum(kh * kh, axis=-1, keepdims=True) + EPS)
        beta = bg[:, h:h + 1]
        gc = gc_all[:, GDN_HEADS + h:GDN_HEADS + h + 1]
        gc_row = gc_rows[GDN_HEADS + h:GDN_HEADS + h + 1, :]
        decay = jnp.where(causal, jnp.exp(jnp.where(causal, gc - gc_row, 0.0)), 0.0)
        kbh = kh * beta
        kh16 = kh.astype(BF16)
        kk.append(jnp.where(
            strict, lax.dot_general(kbh.astype(BF16), kh16, nt, preferred_element_type=F32) * decay, 0.0))
        qk.append(lax.dot_general(qh.astype(BF16), kh16, nt, preferred_element_type=F32) * decay)
        e = jnp.exp(gc)
        rhs.append(jnp.concatenate([v_all[:, cols[h]] * beta, kbh * e], axis=1))
        q.append(qh)
        k.append(kh)
        e_gc.append(e)
        grest.append(grest_all[:, GDN_HEADS + h:GDN_HEADS + h + 1])

    pinv = [eye - kk[h] for h in heads]
    power = kk
    for _ in range(5):
        power = [_dot(power[h], power[h]) for h in heads]
        pinv = [pinv[h] + _dot(pinv[h], power[h]) for h in heads]
    uw = [_dot(pinv[h], rhs[h]) for h in heads]
    q_dec = [q[h] * e_gc[h] for h in heads]
    k_dec = [k[h] * jnp.exp(grest[h]) for h in heads]
    g_tot = [jnp.exp(gc_all[:, GDN_HEADS + h:GDN_HEADS + h + 1] + grest[h]) for h in heads]

    state = [state_ref[h] for h in heads]
    outs = [[] for _ in heads]
    for c in range(nchunk):
        rows = slice(c * GDN_CHUNK, (c + 1) * GDN_CHUNK)
        ws = [_dot(jnp.concatenate([uw[h][rows, GDN_DV:], q_dec[h][rows]], axis=0), state[h])
              for h in heads]
        v_new = [uw[h][rows, :GDN_DV] - ws[h][:GDN_CHUNK] for h in heads]
        for h in heads:
            outs[h].append(ws[h][GDN_CHUNK:] + _dot(qk[h][rows, rows], v_new[h]))
        upd = [lax.dot_general(k_dec[h][rows].astype(BF16), v_new[h].astype(BF16), tn,
                               preferred_element_type=F32) for h in heads]
        state = [state[h] * g_tot[h][c * GDN_CHUNK:c * GDN_CHUNK + 1, :] + upd[h] for h in heads]
    for h in heads:
        state_ref[h] = state[h]
        o = jnp.concatenate(outs[h], axis=0)
        var = jnp.mean(o * o, axis=-1, keepdims=True)
        o = o * lax.rsqrt(var + EPS) * nw_ref[...] * _silu(z_ref[:, cols[h]])
        out_ref[:, cols[h]] = o.astype(out_ref.dtype)


def _gdn(main, small, batch, seq, cwq, cwk, cwv, arow, dtrow, nw):
    ns = seq // TS_GDN
    col = lambda c: pl.BlockSpec((TS_GDN, GDN_QK_WIDTH), lambda b, s: (b * ns + s, c))
    return pl.pallas_call(
        _gdn_kernel,
        grid=(batch, ns),
        in_specs=[
            col(2), col(3), col(4), col(5),
            pl.BlockSpec((TS_GDN, LANES), lambda b, s: (b * ns + s, 0)),
            _const_spec((CONV_WIDTH, GDN_QK_WIDTH)),
            _const_spec((CONV_WIDTH, GDN_QK_WIDTH)),
            _const_spec((CONV_WIDTH, GDN_V_WIDTH)),
            _const_spec((1, LANES)),
            _const_spec((1, LANES)),
            _const_spec((1, GDN_DV)),
        ],
        out_specs=pl.BlockSpec((TS_GDN, GDN_V_WIDTH), lambda b, s: (b * ns + s, 0)),
        out_shape=jax.ShapeDtypeStruct((batch * seq, GDN_V_WIDTH), BF16),
        scratch_shapes=[
            pltpu.VMEM((TS_GDN + SUBLANES, GDN_QK_WIDTH), F32),
            pltpu.VMEM((TS_GDN + SUBLANES, GDN_QK_WIDTH), F32),
            pltpu.VMEM((TS_GDN + SUBLANES, GDN_V_WIDTH), F32),
            pltpu.VMEM((GDN_HEADS, GDN_DK, GDN_DV), F32),
        ],
        compiler_params=pltpu.CompilerParams(
            dimension_semantics=("arbitrary", "arbitrary"), vmem_limit_bytes=VMEM_LIMIT),
        name="gdn",
    )(main, main, main, main, small, cwq, cwk, cwv, arow, dtrow, nw)


def _ffn_kernel(x_ref, ya_ref, yb_ref, p_ref, woa_ref, wob_ref, nf_ref, wg_ref, wu_ref, wd_ref,
                np_ref, wpg_ref, bpg_ref, wpp_ref, nfin_ref, out_ref, *, final_norm):
    x = x_ref[...]
    x = x + jnp.dot(ya_ref[...], woa_ref[...], preferred_element_type=F32) \
          + jnp.dot(yb_ref[...], wob_ref[...], preferred_element_type=F32)
    h = _rmsnorm(x, nf_ref[...]).astype(BF16)
    g = jnp.dot(h, wg_ref[...], preferred_element_type=F32)
    u = jnp.dot(h, wu_ref[...], preferred_element_type=F32)
    x = x + _dot(_silu(g) * u, wd_ref[...])
    h = _rmsnorm(x, np_ref[...]).astype(BF16)
    gate = _sigmoid(jnp.dot(h, wpg_ref[...], preferred_element_type=F32) + bpg_ref[...])
    x = x + gate * _dot(p_ref[...], wpp_ref[...])
    out_ref[...] = _rmsnorm(x, nfin_ref[...]) if final_norm else x


def _ffn(x2, ya, yb, p2, woa, wob, nf, wg, wu, wd, npw, wpg, bpg, wpp, nfin, final_norm):
    t = x2.shape[0]
    tile = lambda c: pl.BlockSpec((TM_FFN, c), lambda i: (i, 0))
    return pl.pallas_call(
        functools.partial(_ffn_kernel, final_norm=final_norm),
        grid=(t // TM_FFN,),
        in_specs=[
            tile(D_MODEL), tile(RG_WIDTH), tile(GDN_V_WIDTH), tile(PLE_DIM),
            _const_spec((RG_WIDTH, D_MODEL)), _const_spec((GDN_V_WIDTH, D_MODEL)),
            _const_spec((1, D_MODEL)),
            _const_spec((D_MODEL, D_FF)), _const_spec((D_MODEL, D_FF)), _const_spec((D_FF, D_MODEL)),
            _const_spec((1, D_MODEL)),
            _const_spec((D_MODEL, D_MODEL)), _const_spec((1, D_MODEL)),
            _const_spec((PLE_DIM, D_MODEL)),
            _const_spec((1, D_MODEL)),
        ],
        out_specs=tile(D_MODEL),
        out_shape=jax.ShapeDtypeStruct((t, D_MODEL), F32),
        compiler_params=pltpu.CompilerParams(
            dimension_semantics=("arbitrary",), vmem_limit_bytes=VMEM_LIMIT),
        name="ffn",
    )(x2, ya, yb, p2, woa, wob, nf, wg, wu, wd, npw, wpg, bpg, wpp, nfin)


def _block_diag_halves(w):
    nb, d, _ = w.shape
    per = nb // 2
    eye = jnp.eye(per, dtype=w.dtype)
    w4 = w.reshape(2, per, d, d)
    return jnp.einsum("hbij,bc->hbicj", w4, eye).reshape(2, per * d, per * d)


def kernel(x, p, norm_mix_w, w_in, conv_a_w, conv_a_b, rg_wx, rg_bx, rg_wa, rg_ba, rg_lambda,
           conv_qkv_w, gdn_a_log, gdn_dt_bias, gdn_norm_w, w_out, norm_ffn_w, w_gate, w_up, w_down,
           norm_ple_w, w_ple_gate, b_ple_gate, w_ple_proj, norm_final_w):
    batch, seq, _ = x.shape
    depth = w_in.shape[0]
    t = batch * seq
    x2 = x.reshape(t, D_MODEL)
    row = lambda v: v.reshape(1, -1).astype(F32)
    for i in range(depth):
        w_main = w_in[i][:, :MAIN_COLS].astype(BF16)
        w_small = jnp.pad(w_in[i][:, MAIN_COLS:], ((0, 0), (0, LANES - 2 * GDN_HEADS))).astype(BF16)
        main, small = _inproj(x2, row(norm_mix_w[i]), w_main, w_small)

        ya = _rglru(main, batch, seq, conv_a_w[i], row(conv_a_b[i]),
                    _block_diag_halves(rg_wx[i]).astype(BF16), _block_diag_halves(rg_wa[i]).astype(BF16),
                    row(rg_bx[i]), row(rg_ba[i]), row(rg_lambda[i]))

        cw = conv_qkv_w[i]
        arow = jnp.pad(gdn_a_log[i], (GDN_HEADS, LANES - 2 * GDN_HEADS)).reshape(1, LANES)
        dtrow = jnp.pad(gdn_dt_bias[i], (GDN_HEADS, LANES - 2 * GDN_HEADS)).reshape(1, LANES)
        yb = _gdn(main, small, batch, seq,
                  cw[:, :GDN_QK_WIDTH], cw[:, GDN_QK_WIDTH:2 * GDN_QK_WIDTH], cw[:, 2 * GDN_QK_WIDTH:],
                  arow, dtrow, row(gdn_norm_w[i]))

        x2 = _ffn(x2, ya, yb, p[i].reshape(t, PLE_DIM),
                  w_out[i][:RG_WIDTH].astype(BF16), w_out[i][RG_WIDTH:].astype(BF16),
                  row(norm_ffn_w[i]), w_gate[i].astype(BF16), w_up[i].astype(BF16), w_down[i].astype(BF16),
                  row(norm_ple_w[i]), w_ple_gate[i].astype(BF16), row(b_ple_gate[i]),
                  w_ple_proj[i].astype(BF16), row(norm_final_w), i == depth - 1)
    return x2.reshape(batch, seq, D_MODEL)
```

```python
import functools

import jax
import jax.numpy as jnp
from jax import lax
from jax.experimental import pallas as pl
from jax.experimental.pallas import tpu as pltpu

F32 = jnp.float32
BF16 = jnp.bfloat16

D_MODEL = 1024
PLE_DIM = 256
CONV_WIDTH = 4
RG_WIDTH = D_MODEL // 2
RG_BLOCKS = 8
RG_C = 8.0
GDN_HEADS = 4
GDN_DK = 128
GDN_DV = 128
GDN_QK_WIDTH = GDN_HEADS * GDN_DK
GDN_V_WIDTH = GDN_HEADS * GDN_DV
GDN_CHUNK = 64
MAIN_COLS = 2 * RG_WIDTH + 2 * GDN_QK_WIDTH + 2 * GDN_V_WIDTH
CONV_COLS = RG_WIDTH + 2 * GDN_QK_WIDTH + GDN_V_WIDTH
LANES = 128
SUBLANES = 8
D_FF = -(-8 * D_MODEL // (3 * 256)) * 256
EPS = 1e-6

MAIN_DTYPE = F32
VMEM_LIMIT = 56 * 1024 * 1024

TM_PROJ = 512
PROJ_CHUNK = 512
TS_RG = 512
RG_SEGS = SUBLANES
RG_SEG_ROWS = TS_RG // RG_SEGS
RG_SEG_PITCH = RG_SEG_ROWS + 4
TS_GDN = 256
GDN_NB = 4
TM_FFN = 512


def _dot(a, b):
    return jnp.dot(a.astype(BF16), b.astype(BF16), preferred_element_type=F32)


def _chunk_cumsum(x):
    rows = x.shape[0]
    pos = lax.broadcasted_iota(jnp.int32, x.shape, 0) % GDN_CHUNK
    shift = 1
    while shift < GDN_CHUNK:
        if shift < SUBLANES:
            moved = pltpu.roll(x, shift, 0)
        else:
            moved = jnp.concatenate([jnp.zeros((shift, x.shape[1]), x.dtype), x[:rows - shift]], axis=0)
        x = x + jnp.where(pos >= shift, moved, 0.0)
        shift *= 2
    return x


def _rmsnorm(x, w):
    var = jnp.mean(x * x, axis=-1, keepdims=True)
    return x * lax.rsqrt(var + EPS) * w


def _sigmoid(x):
    return 1.0 / (1.0 + jnp.exp(-x))


def _silu(x):
    return x * _sigmoid(x)


def _softplus(x):
    return jnp.maximum(x, 0.0) + jnp.log1p(jnp.exp(-jnp.abs(x)))


def _gelu_tanh(x):
    c = 0.7978845608028654
    half = 0.5 * x
    return half + half * jnp.tanh(x * (c + (c * 0.044715) * (x * x)))


def _causal_conv(xbuf_ref, zbuf_ref, slab0, x, w_ref, ts):
    outs = []
    for l in range(x.shape[1] // LANES):
        lanes = slice(l * LANES, (l + 1) * LANES)
        xs = x[:, lanes]
        xbuf_ref[slab0 + l, pl.ds(SUBLANES, ts), :] = xs
        sx = xbuf_ref[slab0 + l, pl.ds(SUBLANES - 1, ts, stride=1), :]
        z = xs * w_ref[1:2, lanes] + sx * w_ref[0:1, lanes]
        zbuf_ref[slab0 + l, pl.ds(SUBLANES, ts), :] = z
        sz = zbuf_ref[slab0 + l, pl.ds(SUBLANES - 2, ts, stride=1), :]
        outs.append(xs * w_ref[3:4, lanes] + sx * w_ref[2:3, lanes] + sz)
        xbuf_ref[slab0 + l, pl.ds(0, SUBLANES), :] = xs[ts - SUBLANES:, :]
        zbuf_ref[slab0 + l, pl.ds(0, SUBLANES), :] = z[ts - SUBLANES:, :]
    return jnp.concatenate(outs, axis=1)


def _const_spec(shape):
    nd = len(shape)
    return pl.BlockSpec(shape, lambda *_: (0,) * nd, pipeline_mode=pl.Buffered(1))


def _inproj_kernel(x_ref, nw_ref, wm_ref, ws_ref, cw_ref, cb_ref, arow_ref, dtrow_ref,
                   main_ref, small_ref, xbuf_ref, zbuf_ref, h_ref, *, tiles_per_seq):
    tm = TM_PROJ

    @pl.when(pl.program_id(0) % tiles_per_seq == 0)
    def _():
        zero = jnp.zeros((CONV_COLS // LANES, SUBLANES, LANES), F32)
        xbuf_ref[:, pl.ds(0, SUBLANES), :] = zero
        zbuf_ref[:, pl.ds(0, SUBLANES), :] = zero

    h_ref[...] = _rmsnorm(x_ref[...], nw_ref[...]).astype(BF16)
    width = PROJ_CHUNK
    per_group = RG_WIDTH // width

    def proj(c):
        return jnp.dot(h_ref[...], wm_ref[:, c * width:(c + 1) * width], preferred_element_type=F32)

    def conv(acc, start):
        cslot = slice(start, start + width)
        return _causal_conv(xbuf_ref, zbuf_ref, start // LANES, acc, cw_ref.at[:, cslot], tm)

    def finish(c, acc):
        g = c // per_group
        cols = slice(c * width, (c + 1) * width)
        if g == 0:
            main_ref[:, cols] = (conv(acc, cols.start) + cb_ref[:, cols]).astype(main_ref.dtype)
        elif g in (1, 5):
            main_ref[:, cols] = acc.astype(main_ref.dtype)
        elif g == 4:
            main_ref[:, cols] = _silu(conv(acc, cols.start - RG_WIDTH)).astype(main_ref.dtype)
        else:
            t_all = _silu(conv(acc, cols.start - RG_WIDTH))
            for hd in range(width // GDN_DK):
                t = t_all[:, hd * GDN_DK:(hd + 1) * GDN_DK]
                scale = lax.rsqrt(jnp.sum(t * t, axis=-1, keepdims=True) + EPS)
                if g == 2:
                    scale = scale * (GDN_DK ** -0.5)
                main_ref[:, cols.start + hd * GDN_DK:cols.start + (hd + 1) * GDN_DK] = (
                    t * scale).astype(main_ref.dtype)

    heavy = [c for c in range(MAIN_COLS // width) if c // per_group in (0, 2, 3, 4)]
    light = [c for c in range(MAIN_COLS // width) if c // per_group in (1, 5)]
    order = []
    while heavy or light:
        order += heavy[:2] + light[:1]
        heavy, light = heavy[2:], light[1:]
    acc = proj(order[0])
    for n, c in enumerate(order):
        nxt = proj(order[n + 1]) if n + 1 < len(order) else None
        finish(c, acc)
        acc = nxt

    sm = jnp.dot(h_ref[...], ws_ref[...], preferred_element_type=F32)
    lane = lax.broadcasted_iota(jnp.int32, sm.shape, 1)
    small_ref[...] = jnp.where(lane < GDN_HEADS, _sigmoid(sm),
                               -jnp.exp(arow_ref[...]) * _softplus(sm + dtrow_ref[...]))


def _inproj(x2, seq, nw, w_main, w_small, cw, cb, arow, dtrow):
    t = x2.shape[0]
    return pl.pallas_call(
        functools.partial(_inproj_kernel, tiles_per_seq=seq // TM_PROJ),
        grid=(t // TM_PROJ,),
        in_specs=[
            pl.BlockSpec((TM_PROJ, D_MODEL), lambda i: (i, 0)),
            _const_spec((1, D_MODEL)),
            _const_spec((D_MODEL, MAIN_COLS)),
            _const_spec((D_MODEL, LANES)),
            _const_spec((CONV_WIDTH, CONV_COLS)),
            _const_spec((1, RG_WIDTH)),
            _const_spec((1, LANES)),
            _const_spec((1, LANES)),
        ],
        out_specs=[
            pl.BlockSpec((TM_PROJ, MAIN_COLS), lambda i: (i, 0)),
            pl.BlockSpec((TM_PROJ, LANES), lambda i: (i, 0)),
        ],
        out_shape=[
            jax.ShapeDtypeStruct((t, MAIN_COLS), MAIN_DTYPE),
            jax.ShapeDtypeStruct((t, LANES), F32),
        ],
        scratch_shapes=[pltpu.VMEM((CONV_COLS // LANES, TM_PROJ + SUBLANES, LANES), F32),
                        pltpu.VMEM((CONV_COLS // LANES, TM_PROJ + SUBLANES, LANES), F32),
                        pltpu.VMEM((TM_PROJ, D_MODEL), BF16)],
        compiler_params=pltpu.CompilerParams(
            dimension_semantics=("arbitrary",), vmem_limit_bytes=VMEM_LIMIT),
        name="inproj",
    )(x2, nw, w_main, w_small, cw, cb, arow, dtrow)


def _rglru_kernel(u_ref, ga_ref, wx_ref, wa_ref, bx_ref, ba_ref, lam_ref,
                  out_ref, h_ref, a_ref, b_ref):
    ts = TS_RG
    s = pl.program_id(1)

    @pl.when(s == 0)
    def _():
        h_ref[...] = jnp.zeros((SUBLANES, RG_WIDTH), F32)

    u = u_ref[...].astype(F32)
    ub = u.astype(BF16)
    half = RG_WIDTH // 2

    def gate(w_ref, b_ref_):
        lo = jnp.dot(ub[:, :half], w_ref[0], preferred_element_type=F32)
        hi = jnp.dot(ub[:, half:], w_ref[1], preferred_element_type=F32)
        return _sigmoid(jnp.concatenate([lo, hi], axis=1) + b_ref_[...])

    gate_x = gate(wx_ref, bx_ref)
    gate_a = gate(wa_ref, ba_ref)
    log_a = (-RG_C) * gate_a * _softplus(-lam_ref[...])
    a = jnp.exp(log_a)
    m2 = jnp.maximum(-jnp.tanh(log_a) * (a * a + 1.0), 0.0)
    mult = jnp.where(m2 == 0.0, 0.0, m2 * lax.rsqrt(m2))
    row = lax.broadcasted_iota(jnp.int32, (ts, 1), 0)
    mult = jnp.where(jnp.logical_and(row == 0, s == 0), 1.0, mult)
    bt = u * gate_x * mult

    slabs = range(RG_WIDTH // LANES)
    for l in slabs:
        lanes = slice(l * LANES, (l + 1) * LANES)
        for sg in range(RG_SEGS):
            rows = slice(sg * RG_SEG_ROWS, (sg + 1) * RG_SEG_ROWS)
            dst = pl.ds(sg * RG_SEG_PITCH, RG_SEG_ROWS, stride=1)
            a_ref[l, dst, :] = a[rows, lanes]
            b_ref[l, dst, :] = bt[rows, lanes]

    def scan_step(j, carry):
        hs, acs = carry
        across = pl.ds(j, RG_SEGS, stride=RG_SEG_PITCH)
        new_h, new_ac = [], []
        for l in slabs:
            aj = a_ref[l, across, :]
            hj = aj * hs[l] + b_ref[l, across, :]
            acj = aj * acs[l]
            b_ref[l, across, :] = hj
            a_ref[l, across, :] = acj
            new_h.append(hj)
            new_ac.append(acj)
        return tuple(new_h), tuple(new_ac)

    zero = jnp.zeros((RG_SEGS, LANES), F32)
    one = jnp.ones((RG_SEGS, LANES), F32)
    h_last, ac_last = lax.fori_loop(0, RG_SEG_ROWS, scan_step,
                                    (tuple(zero for _ in slabs), tuple(one for _ in slabs)), unroll=8)

    sub = lax.broadcasted_iota(jnp.int32, (RG_SEGS, LANES), 0)
    for l in slabs:
        lanes = slice(l * LANES, (l + 1) * LANES)
        seg_a, seg_b = ac_last[l], h_last[l]
        for sh in (1, 2, 4):
            m = sub >= sh
            seg_b = jnp.where(m, seg_a * pltpu.roll(seg_b, sh, 0) + seg_b, seg_b)
            seg_a = jnp.where(m, seg_a * pltpu.roll(seg_a, sh, 0), seg_a)
        carry = h_ref[:, lanes]
        h_end = seg_b + seg_a * carry
        h_in = jnp.where(sub == 0, carry, pltpu.roll(h_end, 1, 0))
        h_ref[:, lanes] = jnp.broadcast_to(h_end[RG_SEGS - 1:, :], (RG_SEGS, LANES))
        for sg in range(RG_SEGS):
            rows = slice(sg * RG_SEG_ROWS, (sg + 1) * RG_SEG_ROWS)
            src = pl.ds(sg * RG_SEG_PITCH, RG_SEG_ROWS, stride=1)
            h_true = b_ref[l, src, :] + a_ref[l, src, :] * h_in[sg:sg + 1, :]
            out_ref[rows, lanes] = (h_true * _gelu_tanh(ga_ref[rows, lanes].astype(F32))).astype(out_ref.dtype)


def _rglru(main, batch, seq, wx, wa, bx, ba, lam):
    ns = seq // TS_RG
    vec = lambda: _const_spec((1, RG_WIDTH))
    return pl.pallas_call(
        _rglru_kernel,
        grid=(batch, ns),
        in_specs=[
            pl.BlockSpec((TS_RG, RG_WIDTH), lambda b, s: (b * ns + s, 0)),
            pl.BlockSpec((TS_RG, RG_WIDTH), lambda b, s: (b * ns + s, 1)),
            _const_spec((2, RG_WIDTH // 2, RG_WIDTH // 2)),
            _const_spec((2, RG_WIDTH // 2, RG_WIDTH // 2)),
            vec(), vec(), vec(),
        ],
        out_specs=pl.BlockSpec((TS_RG, RG_WIDTH), lambda b, s: (b * ns + s, 0)),
        out_shape=jax.ShapeDtypeStruct((batch * seq, RG_WIDTH), BF16),
        scratch_shapes=[
            pltpu.VMEM((SUBLANES, RG_WIDTH), F32),
            pltpu.VMEM((RG_WIDTH // LANES, RG_SEGS * RG_SEG_PITCH, LANES), F32),
            pltpu.VMEM((RG_WIDTH // LANES, RG_SEGS * RG_SEG_PITCH, LANES), F32),
        ],
        compiler_params=pltpu.CompilerParams(
            dimension_semantics=("arbitrary", "arbitrary"), vmem_limit_bytes=VMEM_LIMIT),
        name="rglru",
    )(main, main, wx, wa, bx, ba, lam)


def _gdn_kernel(q_ref, k_ref, v_ref, z_ref, bg_ref, nw_ref, out_ref, state_ref):
    ts = TS_GDN
    nchunk = ts // GDN_CHUNK
    s = pl.program_id(1)

    @pl.when(s == 0)
    def _():
        state_ref[...] = jnp.zeros(state_ref.shape, F32)

    ri = lax.broadcasted_iota(jnp.int32, (ts, ts), 0)
    ci = lax.broadcasted_iota(jnp.int32, (ts, ts), 1)
    same = (ri // GDN_CHUNK) == (ci // GDN_CHUNK)
    causal = jnp.logical_and(same, ri >= ci)
    strict = jnp.logical_and(same, ri > ci)
    eye = (ri == ci).astype(F32)
    nt =(((1,), (1,)), ((), ()))
    tn = (((0,), (0,)), ((), ()))

    probs = [(b, h) for b in range(GDN_NB) for h in range(GDN_HEADS)]
    n = range(len(probs))
    cols = [slice(h * GDN_DK, (h + 1) * GDN_DK) for _, h in probs]

    bg = [bg_ref[b] for b in range(GDN_NB)]
    gc_all = [_chunk_cumsum(bg[b]) for b in range(GDN_NB)]
    gtot_all = [jnp.broadcast_to(gc_all[b].reshape(nchunk, GDN_CHUNK, LANES)[:, GDN_CHUNK - 1:, :],
                                 (nchunk, GDN_CHUNK, LANES)).reshape(ts, LANES) for b in range(GDN_NB)]
    grest_all = [gtot_all[b] - gc_all[b] for b in range(GDN_NB)]
    gc_rows = [gc_all[b].T for b in range(GDN_NB)]

    q, k, kk, qk, rhs, e_gc, g_tot, e_rest = [], [], [], [], [], [], [], []
    for i, (b, h) in enumerate(probs):
        qh = q_ref[b, :, cols[i]].astype(F32)
        kh = k_ref[b, :, cols[i]].astype(F32)
        beta = bg[b][:, h:h + 1]
        gc = gc_all[b][:, GDN_HEADS + h:GDN_HEADS + h + 1]
        grest = grest_all[b][:, GDN_HEADS + h:GDN_HEADS + h + 1]
        gc_row = gc_rows[b][GDN_HEADS + h:GDN_HEADS + h + 1, :]
        decay = jnp.where(causal, jnp.exp(jnp.where(causal, gc - gc_row, 0.0)), 0.0)
        kbh = kh * beta
        kh16 = kh.astype(BF16)
        kk.append(jnp.where(
            strict, lax.dot_general(kbh.astype(BF16), kh16, nt, preferred_element_type=F32) * decay, 0.0))
        qk.append(lax.dot_general(qh.astype(BF16), kh16, nt, preferred_element_type=F32) * decay)
        e = jnp.exp(gc)
        rhs.append(jnp.concatenate([v_ref[b, :, cols[i]].astype(F32) * beta, kbh * e], axis=1))
        q.append(qh)
        k.append(kh)
        e_gc.append(e)
        e_rest.append(jnp.exp(grest))
        g_tot.append(jnp.exp(gtot_all[b][:, GDN_HEADS + h:GDN_HEADS + h + 1]))

    pinv = [eye - kk[i] for i in n]
    power = kk
    for _ in range(5):
        power = [_dot(power[i], power[i]) for i in n]
        pinv = [pinv[i] + _dot(pinv[i], power[i]) for i in n]
    uw = [_dot(pinv[i], rhs[i]) for i in n]
    q_dec = [q[i] * e_gc[i] for i in n]
    k_dec = [k[i] * e_rest[i] for i in n]

    state = [state_ref[i] for i in n]
    outs = [[] for _ in n]
    for c in range(nchunk):
        rows = slice(c * GDN_CHUNK, (c + 1) * GDN_CHUNK)
        ws = [_dot(jnp.concatenate([uw[i][rows, GDN_DV:], q_dec[i][rows]], axis=0), state[i])
              for i in n]
        v_new = [uw[i][rows, :GDN_DV] - ws[i][:GDN_CHUNK] for i in n]
        for i in n:
            outs[i].append(ws[i][GDN_CHUNK:] + _dot(qk[i][rows, rows], v_new[i]))
        upd = [lax.dot_general(k_dec[i][rows].astype(BF16), v_new[i].astype(BF16), tn,
                               preferred_element_type=F32) for i in n]
        state = [state[i] * g_tot[i][c * GDN_CHUNK:c * GDN_CHUNK + 1, :] + upd[i] for i in n]
    for i, (b, h) in enumerate(probs):
        state_ref[i] = state[i]
        o = jnp.concatenate(outs[i], axis=0)
        var = jnp.mean(o * o, axis=-1, keepdims=True)
        o = o * lax.rsqrt(var + EPS) * nw_ref[...] * _silu(z_ref[b, :, cols[i]].astype(F32))
        out_ref[b, :, cols[i]] = o.astype(out_ref.dtype)


def _gdn(main, small, batch, seq, nw):
    main3 = main.reshape(batch, seq, MAIN_COLS)
    col = lambda c: pl.BlockSpec((GDN_NB, TS_GDN, GDN_QK_WIDTH), lambda b, s: (b, s, c))
    out = pl.pallas_call(
        _gdn_kernel,
        grid=(batch // GDN_NB, seq // TS_GDN),
        in_specs=[
            col(2), col(3), col(4), col(5),
            pl.BlockSpec((GDN_NB, TS_GDN, LANES), lambda b, s: (b, s, 0)),
            _const_spec((1, GDN_DV)),
        ],
        out_specs=pl.BlockSpec((GDN_NB, TS_GDN, GDN_V_WIDTH), lambda b, s: (b, s, 0)),
        out_shape=jax.ShapeDtypeStruct((batch, seq, GDN_V_WIDTH), BF16),
        scratch_shapes=[pltpu.VMEM((GDN_NB * GDN_HEADS, GDN_DK, GDN_DV), F32)],
        compiler_params=pltpu.CompilerParams(
            dimension_semantics=("arbitrary", "arbitrary"), vmem_limit_bytes=VMEM_LIMIT),
        name="gdn",
    )(main3, main3, main3, main3, small.reshape(batch, seq, LANES), nw)
    return out.reshape(batch * seq, GDN_V_WIDTH)


def _ffn_kernel(x_ref, ya_ref, yb_ref, p_ref, woa_ref, wob_ref, nf_ref, wg_ref, wu_ref, wd_ref,
                np_ref, wpg_ref, bpg_ref, wpp_ref, nfin_ref, out_ref, *, final_norm):
    x = x_ref[...]
    x = x + jnp.dot(ya_ref[...], woa_ref[...], preferred_element_type=F32) \
          + jnp.dot(yb_ref[...], wob_ref[...], preferred_element_type=F32)
    h = _rmsnorm(x, nf_ref[...]).astype(BF16)
    g = jnp.dot(h, wg_ref[...], preferred_element_type=F32)
    u = jnp.dot(h, wu_ref[...], preferred_element_type=F32)
    x = x + _dot(_silu(g) * u, wd_ref[...])
    h = _rmsnorm(x, np_ref[...]).astype(BF16)
    gate = _sigmoid(jnp.dot(h, wpg_ref[...], preferred_element_type=F32) + bpg_ref[...])
    x = x + gate * _dot(p_ref[...], wpp_ref[...])
    out_ref[...] = _rmsnorm(x, nfin_ref[...]) if final_norm else x


def _ffn(x2, ya, yb, p2, woa, wob, nf, wg, wu, wd, npw, wpg, bpg, wpp, nfin, final_norm):
    t = x2.shape[0]
    tile = lambda c: pl.BlockSpec((TM_FFN, c), lambda i: (i, 0))
    return pl.pallas_call(
        functools.partial(_ffn_kernel, final_norm=final_norm),
        grid=(t // TM_FFN,),
        in_specs=[
            tile(D_MODEL), tile(RG_WIDTH), tile(GDN_V_WIDTH), tile(PLE_DIM),
            _const_spec((RG_WIDTH, D_MODEL)), _const_spec((GDN_V_WIDTH, D_MODEL)),
            _const_spec((1, D_MODEL)),
            _const_spec((D_MODEL, D_FF)), _const_spec((D_MODEL, D_FF)), _const_spec((D_FF, D_MODEL)),
            _const_spec((1, D_MODEL)),
            _const_spec((D_MODEL, D_MODEL)), _const_spec((1, D_MODEL)),
            _const_spec((PLE_DIM, D_MODEL)),
            _const_spec((1, D_MODEL)),
        ],
        out_specs=tile(D_MODEL),
        out_shape=jax.ShapeDtypeStruct((t, D_MODEL), F32),
        compiler_params=pltpu.CompilerParams(
            dimension_semantics=("arbitrary",), vmem_limit_bytes=VMEM_LIMIT),
        name="ffn",
    )(x2, ya, yb, p2, woa, wob, nf, wg, wu, wd, npw, wpg, bpg, wpp, nfin)


def _block_diag_halves(w):
    nb, d, _ = w.shape
    per = nb // 2
    eye = jnp.eye(per, dtype=w.dtype)
    w4 = w.reshape(2, per, d, d)
    return jnp.einsum("hbij,bc->hbicj", w4, eye).reshape(2, per * d, per * d)


def kernel(x, p, norm_mix_w, w_in, conv_a_w, conv_a_b, rg_wx, rg_bx, rg_wa, rg_ba, rg_lambda,
           conv_qkv_w, gdn_a_log, gdn_dt_bias, gdn_norm_w, w_out, norm_ffn_w, w_gate, w_up, w_down,
           norm_ple_w, w_ple_gate, b_ple_gate, w_ple_proj, norm_final_w):
    batch, seq, _ = x.shape
    depth = w_in.shape[0]
    t = batch * seq
    x2 = x.reshape(t, D_MODEL)
    row = lambda v: v.reshape(1, -1).astype(F32)
    for i in range(depth):
        w_main = w_in[i][:, :MAIN_COLS].astype(BF16)
        w_small = jnp.pad(w_in[i][:, MAIN_COLS:], ((0, 0), (0, LANES - 2 * GDN_HEADS))).astype(BF16)
        cw = jnp.concatenate([conv_a_w[i], conv_qkv_w[i]], axis=1)
        arow = jnp.pad(gdn_a_log[i], (GDN_HEADS, LANES - 2 * GDN_HEADS)).reshape(1, LANES)
        dtrow = jnp.pad(gdn_dt_bias[i], (GDN_HEADS, LANES - 2 * GDN_HEADS)).reshape(1, LANES)
        main, small = _inproj(x2, seq, row(norm_mix_w[i]), w_main, w_small, cw, row(conv_a_b[i]), arow, dtrow)

        ya = _rglru(main, batch, seq,
                    _block_diag_halves(rg_wx[i]).astype(BF16), _block_diag_halves(rg_wa[i]).astype(BF16),
                    row(rg_bx[i]), row(rg_ba[i]), row(rg_lambda[i]))
        yb = _gdn(main, small, batch, seq, row(gdn_norm_w[i]))

        x2 = _ffn(x2, ya, yb, p[i].reshape(t, PLE_DIM),
                  w_out[i][:RG_WIDTH].astype(BF16), w_out[i][RG_WIDTH:].astype(BF16),
                  row(norm_ffn_w[i]), w_gate[i].astype(BF16), w_up[i].astype(BF16), w_down[i].astype(BF16),
                  row(norm_ple_w[i]), w_ple_gate[i].astype(BF16), row(b_ple_gate[i]),
                  w_ple_proj[i].astype(BF16), row(norm_final_w), i == depth - 1)
    return x2.reshape(batch, seq, D_MODEL)
```
